```python
import jax
import jax.numpy as jnp
from jax import lax
import numpy as np

D_MODEL = 1024
BATCH = 4
SEQ = 8192
DEPTH = 4

N_MIXERS = 4
NORM_EPS = 1e-6
ROPE_THETA = 10000.0
ROPE_DIM = 64
NEG_INF = -1e30
PLE_DIM = 256

MLA_HEADS = 8
MLA_Q_LORA = 384
MLA_KV_LORA = 256
MLA_NOPE = 128
MLA_V = 128
Q_BLOCK = 128

FNET_GROUPS = 4

GMLP_HALF = 2 * D_MODEL
GMLP_CHUNK = 128
GMLP_GROUP_DIM = 128
GMLP_GROUPS = GMLP_HALF // GMLP_GROUP_DIM

DIL_PATTERNS = ((128, 1), (512, 4), (2048, 16))
DIL_GROUPS = 3
DIL_HEADS = 8
DIL_HEAD_DIM = ROPE_DIM

MOE_GROUPS = 4
MOE_PER_GROUP = 8
MOE_EXPERTS = MOE_GROUPS * MOE_PER_GROUP
MOE_TOP_K = 2
MOE_HIDDEN = 512
MOE_BLOCK = 256

N_MLA = (DEPTH + 3) // 4
N_FNET = (DEPTH + 2) // 4
N_GMLP = (DEPTH + 1) // 4
N_DIL = DEPTH // 4

kernel_name = 'hybrid_mla_fnet_gmlp_dilated_hmoe'


def rmsnorm(x, g):
    xf = x.astype(jnp.float32)
    y = xf * lax.rsqrt(jnp.mean(xf * xf, axis=-1, keepdims=True) + NORM_EPS)
    return (y * g.astype(jnp.float32)).astype(x.dtype)


def rope_table(seq_len):
    inv = ROPE_THETA ** (-jnp.arange(0, ROPE_DIM, 2, dtype=jnp.float32) / ROPE_DIM)
    ang = jnp.arange(seq_len, dtype=jnp.float32)[:, None] * inv[None, :]
    return jnp.cos(ang), jnp.sin(ang)


def apply_rope(t, cos, sin):
    shp = (cos.shape[0],) + (1,) * (t.ndim - 3) + (cos.shape[1],)
    c, s = cos.reshape(shp), sin.reshape(shp)
    t1, t2 = jnp.split(t.astype(jnp.float32), 2, axis=-1)
    return jnp.concatenate([t1 * c - t2 * s, t2 * c + t1 * s], axis=-1).astype(t.dtype)


def mla_mixer(xn, cos, sin, w_a, norm_q, norm_kv, w_uq, w_ukv, w_o):
    B, S, _ = xn.shape
    a = xn @ w_a
    c_q, c_kv, k_rope = jnp.split(a, [MLA_Q_LORA, MLA_Q_LORA + MLA_KV_LORA], axis=-1)
    q = (rmsnorm(c_q, norm_q) @ w_uq).reshape(B, S, MLA_HEADS, MLA_NOPE + ROPE_DIM)
    q_nope, q_rope = q[..., :MLA_NOPE], apply_rope(q[..., MLA_NOPE:], cos, sin)
    kv = (rmsnorm(c_kv, norm_kv) @ w_ukv).reshape(B, S, MLA_HEADS, MLA_NOPE + MLA_V)
    k_nope, v = kv[..., :MLA_NOPE], kv[..., MLA_NOPE:]
    k_rope = apply_rope(k_rope, cos, sin)
    scale = (MLA_NOPE + ROPE_DIM) ** -0.5
    nq = S // Q_BLOCK
    qn_b = jnp.moveaxis(q_nope.reshape(B, nq, Q_BLOCK, MLA_HEADS, MLA_NOPE), 1, 0)
    qr_b = jnp.moveaxis(q_rope.reshape(B, nq, Q_BLOCK, MLA_HEADS, ROPE_DIM), 1, 0)

    def attend(blk):
        qn, qr = blk
        s = jnp.einsum('bqhd,bkhd->bhqk', qn, k_nope, preferred_element_type=jnp.float32)
        s = s + jnp.einsum('bqhr,bkr->bhqk', qr, k_rope, preferred_element_type=jnp.float32)
        pr = jax.nn.softmax(s * scale, axis=-1).astype(v.dtype)
        return jnp.einsum('bhqk,bkhd->bqhd', pr, v)

    o = lax.map(attend, (qn_b, qr_b))
    o = jnp.moveaxis(o, 0, 1).reshape(B, S, MLA_HEADS * MLA_V)
    return o @ w_o


def fnet_mixer(xn, w_o, b_o):
    B, S, D = xn.shape
    z = xn.astype(jnp.float32).reshape(B, S, FNET_GROUPS, D // FNET_GROUPS)
    f = jnp.fft.fft2(z, axes=(1, 3), norm='ortho').real
    return f.reshape(B, S, D).astype(xn.dtype) @ w_o + b_o


def gmlp_mixer(xn, w_in, norm_v, w_s, b_s, w_o):
    B, S, _ = xn.shape
    z = jax.nn.gelu(xn @ w_in)
    u, v = jnp.split(z, 2, axis=-1)
    v = rmsnorm(v, norm_v).reshape(B, S // GMLP_CHUNK, GMLP_CHUNK, GMLP_GROUPS, GMLP_GROUP_DIM)
    sv = jnp.einsum('gpq,bcqgd->bcpgd', w_s, v) + b_s.T[:, :, None]
    return (u * sv.reshape(B, S, GMLP_HALF)) @ w_o


def local_window_attention(q, k, v, n):
    Bp, L, H, Dh = q.shape
    nb = -(-L // n)
    Lp = nb * n
    qb = jnp.pad(q, ((0, 0), (0, Lp - L), (0, 0), (0, 0))).reshape(Bp, nb, n, H, Dh)

    def band(t):
        tp = jnp.pad(t, ((0, 0), (n, Lp - L + n), (0, 0), (0, 0))).reshape(Bp, nb + 2, n, H, Dh)
        return jnp.concatenate([tp[:, :nb], tp[:, 1:nb + 1], tp[:, 2:]], axis=2)

    kb, vb = band(k), band(v)
    qpos = jnp.arange(Lp).reshape(nb, n)
    kpos = (jnp.arange(nb)[:, None] - 1) * n + jnp.arange(3 * n)[None, :]
    diff = kpos[:, None, :] - qpos[:, :, None]
    valid = (jnp.abs(diff) <= n) & (kpos[:, None, :] >= 0) & (kpos[:, None, :] < L)
    s = jnp.einsum('bcqhd,bckhd->bchqk', qb, kb, preferred_element_type=jnp.float32) * Dh ** -0.5
    s = jnp.where(valid[None, :, None], s, NEG_INF)
    lse = jax.nn.logsumexp(s, axis=-1, keepdims=True)
    o = jnp.einsum('bchqk,bckhd->bcqhd', jnp.exp(s - lse).astype(v.dtype), vb)
    o = o.reshape(Bp, Lp, H, Dh)[:, :L]
    lse = jnp.transpose(lse[..., 0], (0, 1, 3, 2)).reshape(Bp, Lp, H)[:, :L]
    return o, lse


def dilated_group(q, k, v, window, dil):
    B, S, H, Dh = q.shape
    L = S // dil

    def to_cls(t):
        return jnp.transpose(t.reshape(B, L, dil, H, Dh), (0, 2, 1, 3, 4)).reshape(B * dil, L, H, Dh)

    o, lse = local_window_attention(to_cls(q), to_cls(k), to_cls(v), window // (2 * dil))
    o = jnp.transpose(o.reshape(B, dil, L, H, Dh), (0, 2, 1, 3, 4)).reshape(B, S, H, Dh)
    lse = jnp.transpose(lse.reshape(B, dil, L, H), (0, 2, 1, 3)).reshape(B, S, H)
    return o, lse


def dilated_mixer(xn, cos, sin, w_qkv, w_o):
    B, S, _ = xn.shape
    qkv = (xn @ w_qkv).reshape(B, S, DIL_GROUPS, 3, DIL_HEADS, DIL_HEAD_DIM)
    outs, lses = [], []
    for g in range(DIL_GROUPS):
        window, dil = DIL_PATTERNS[g]
        q = apply_rope(qkv[:, :, g, 0], cos, sin)
        k = apply_rope(qkv[:, :, g, 1], cos, sin)
        o, l = dilated_group(q, k, qkv[:, :, g, 2], window, dil)
        outs.append(o)
        lses.append(l)
    alpha = jax.nn.softmax(jnp.stack(lses, axis=0), axis=0)
    o = jnp.sum(alpha[..., None].astype(xn.dtype) * jnp.stack(outs, axis=0), axis=0)
    return o.reshape(B, S, DIL_HEADS * DIL_HEAD_DIM) @ w_o


def hier_moe(xn, w_grp, b_grp, w_exp, b_exp, w_gate, w_up, w_down):
    B, S, D = xn.shape
    N = B * S
    xf = xn.reshape(N, D)
    pg = jax.nn.softmax((xf @ w_grp).astype(jnp.float32) + b_grp.astype(jnp.float32), axis=-1)
    g = jnp.argmax(pg, axis=-1)
    g_gate = jnp.take_along_axis(pg, g[:, None], axis=1)[:, 0]
    le = ((xf @ w_exp).astype(jnp.float32) + b_exp.astype(jnp.float32)).reshape(N, MOE_GROUPS, MOE_PER_GROUP)
    le_g = jnp.take_along_axis(le, g[:, None, None], axis=1)[:, 0]
    top_v, top_i = lax.top_k(le_g, MOE_TOP_K)
    wts = (jax.nn.softmax(top_v, axis=-1) * g_gate[:, None]).reshape(-1)
    eid = (g[:, None] * MOE_PER_GROUP + top_i).reshape(-1).astype(jnp.int32)
    NK = N * MOE_TOP_K
    tok = jnp.arange(NK, dtype=jnp.int32) // MOE_TOP_K
    order = jnp.argsort(eid)
    se, stok, sw = eid[order], tok[order], wts[order]
    counts = jax.ops.segment_sum(jnp.ones((NK,), jnp.int32), eid, num_segments=MOE_EXPERTS)
    padded = ((counts + MOE_BLOCK - 1) // MOE_BLOCK) * MOE_BLOCK
    pad_end = jnp.cumsum(padded)
    pad_start = pad_end - padded
    start = jnp.cumsum(counts) - counts
    dest = pad_start[se] + (jnp.arange(NK, dtype=jnp.int32) - start[se])
    R = ((NK + MOE_EXPERTS * (MOE_BLOCK - 1) + MOE_BLOCK - 1) // MOE_BLOCK) * MOE_BLOCK
    nb = R // MOE_BLOCK
    row_tok = jnp.full((R,), N, jnp.int32).at[dest].set(stok)
    row_w = jnp.zeros((R,), jnp.float32).at[dest].set(sw)
    blk_e = jnp.minimum(jnp.searchsorted(pad_end, jnp.arange(nb) * MOE_BLOCK, side='right'), MOE_EXPERTS - 1)
    x_pad = jnp.concatenate([xf, jnp.zeros((1, D), xf.dtype)], axis=0)

    def run_block(args):
        t, e = args
        xb = x_pad[t]
        hb = jax.nn.silu(xb @ w_gate[e]) * (xb @ w_up[e])
        return hb @ w_down[e]

    y = lax.map(run_block, (row_tok.reshape(nb, MOE_BLOCK), blk_e)).reshape(R, D)
    y = y * row_w[:, None].astype(y.dtype)
    out = jnp.zeros((N + 1, D), y.dtype).at[row_tok].add(y)[:N]
    return out.reshape(B, S, D)


def _dense(key, shape, fan_in):
    return jax.random.normal(key, shape, jnp.float32) * fan_in ** -0.5


def _gain(key, shape):
    return 1.0 + 0.02 * jax.random.normal(key, shape, jnp.float32)


def _small(key, shape):
    return 0.01 * jax.random.normal(key, shape, jnp.float32)


def setup_inputs(seed: int = 0) -> dict:
    key = jax.random.key(seed)
    ks = jax.random.split(key, 36)
    D = D_MODEL
    return {
        'x': jax.random.normal(ks[0], (BATCH, SEQ, D), jnp.float32),
        'p': jax.random.normal(ks[1], (DEPTH, BATCH, SEQ, PLE_DIM), jnp.float32),
        'norm_mix': _gain(ks[2], (DEPTH, D)),
        'norm_moe': _gain(ks[3], (DEPTH, D)),
        'mla_w_a': _dense(ks[4], (N_MLA, D, MLA_Q_LORA + MLA_KV_LORA + ROPE_DIM), D),
        'mla_norm_q': _gain(ks[5], (N_MLA, MLA_Q_LORA)),
        'mla_norm_kv': _gain(ks[6], (N_MLA, MLA_KV_LORA)),
        'mla_w_uq': _dense(ks[7], (N_MLA, MLA_Q_LORA, MLA_HEADS * (MLA_NOPE + ROPE_DIM)), MLA_Q_LORA),
        'mla_w_ukv': _dense(ks[8], (N_MLA, MLA_KV_LORA, MLA_HEADS * (MLA_NOPE + MLA_V)), MLA_KV_LORA),
        'mla_w_o': _dense(ks[9], (N_MLA, MLA_HEADS * MLA_V, D), MLA_HEADS * MLA_V),
        'fnet_w_o': _dense(ks[10], (N_FNET, D, D), D),
        'fnet_b_o': _small(ks[11], (N_FNET, D)),
        'gmlp_w_in': _dense(ks[12], (N_GMLP, D, 2 * GMLP_HALF), D),
        'gmlp_norm_v': _gain(ks[13], (N_GMLP, GMLP_HALF)),
        'gmlp_w_s': _dense(ks[14], (N_GMLP, GMLP_GROUPS, GMLP_CHUNK, GMLP_CHUNK), GMLP_CHUNK),
        'gmlp_b_s': _gain(ks[15], (N_GMLP, GMLP_GROUPS, GMLP_CHUNK)),
        'gmlp_w_o': _dense(ks[16], (N_GMLP, GMLP_HALF, D), GMLP_HALF),
        'dil_w_qkv': _dense(ks[17], (N_DIL, D, DIL_GROUPS * 3 * DIL_HEADS * DIL_HEAD_DIM), D),
        'dil_w_o': _dense(ks[18], (N_DIL, DIL_HEADS * DIL_HEAD_DIM, D), DIL_HEADS * DIL_HEAD_DIM),
        'moe_w_grp': _dense(ks[19], (DEPTH, D, MOE_GROUPS), D),
        'moe_b_grp': _small(ks[20], (DEPTH, MOE_GROUPS)),
        'moe_w_exp': _dense(ks[21], (DEPTH, D, MOE_EXPERTS), D),
        'moe_b_exp': _small(ks[22], (DEPTH, MOE_EXPERTS)),
        'moe_w_gate': _dense(ks[23], (DEPTH, MOE_EXPERTS, D, MOE_HIDDEN), D),
        'moe_w_up': _dense(ks[24], (DEPTH, MOE_EXPERTS, D, MOE_HIDDEN), D),
        'moe_w_down': _dense(ks[25], (DEPTH, MOE_EXPERTS, MOE_HIDDEN, D), MOE_HIDDEN),
        'ple_norm': _gain(ks[26], (DEPTH, D)),
        'ple_w_gate': _dense(ks[27], (DEPTH, D, D), D),
        'ple_w_proj': _dense(ks[28], (DEPTH, PLE_DIM, D), PLE_DIM),
        'norm_final': _gain(ks[29], (D,)),
    }


def reference(x, p, norm_mix, norm_moe, mla_w_a, mla_norm_q, mla_norm_kv, mla_w_uq, mla_w_ukv, mla_w_o,
              fnet_w_o, fnet_b_o, gmlp_w_in, gmlp_norm_v, gmlp_w_s, gmlp_b_s, gmlp_w_o, dil_w_qkv, dil_w_o,
              moe_w_grp, moe_b_grp, moe_w_exp, moe_b_exp, moe_w_gate, moe_w_up, moe_w_down,
              ple_norm, ple_w_gate, ple_w_proj, norm_final):
    S = x.shape[1]
    cos, sin = rope_table(S)
    h = x
    for i in range(DEPTH):
        m, j = i % N_MIXERS, i // N_MIXERS
        hn = rmsnorm(h, norm_mix[i])
        if m == 0:
            y = mla_mixer(hn, cos, sin, mla_w_a[j], mla_norm_q[j], mla_norm_kv[j], mla_w_uq[j], mla_w_ukv[j], mla_w_o[j])
        elif m == 1:
            y = fnet_mixer(hn, fnet_w_o[j], fnet_b_o[j])
        elif m == 2:
            y = gmlp_mixer(hn, gmlp_w_in[j], gmlp_norm_v[j], gmlp_w_s[j], gmlp_b_s[j], gmlp_w_o[j])
        else:
            y = dilated_mixer(hn, cos, sin, dil_w_qkv[j], dil_w_o[j])
        h = h + y
        h = h + hier_moe(rmsnorm(h, norm_moe[i]), moe_w_grp[i], moe_b_grp[i], moe_w_exp[i], moe_b_exp[i],
                         moe_w_gate[i], moe_w_up[i], moe_w_down[i])
        gate = jax.nn.sigmoid(rmsnorm(h, ple_norm[i]) @ ple_w_gate[i])
        h = h + gate * (p[i] @ ple_w_proj[i])
    return rmsnorm(h, norm_final)
```

```python
import functools
import math

import numpy as np
import jax
import jax.numpy as jnp
from jax import lax
from jax.experimental import pallas as pl
from jax.experimental.pallas import tpu as pltpu

D_MODEL = 1024
NORM_EPS = 1e-6
ROPE_THETA = 10000.0
ROPE_DIM = 64
PLE_DIM = 256

MLA_HEADS = 8
MLA_Q_LORA = 384
MLA_KV_LORA = 256
MLA_NOPE = 128
MLA_V = 128
MLA_QK = MLA_NOPE + ROPE_DIM

FNET_GROUPS = 4
FNET_N1 = 64

GMLP_HALF = 2 * D_MODEL
GMLP_CHUNK = 128
GMLP_GROUP_DIM = 128
GMLP_GROUPS = GMLP_HALF // GMLP_GROUP_DIM

DIL_PATTERNS = ((128, 1), (512, 4), (2048, 16))
DIL_GROUPS = 3
DIL_HEADS = 8
DIL_HEAD_DIM = 64
DIL_WIDTH = DIL_HEADS * DIL_HEAD_DIM
DIL_BAND = 64

MOE_GROUPS = 4
MOE_PER_GROUP = 8
MOE_EXPERTS = MOE_GROUPS * MOE_PER_GROUP
MOE_TOP_K = 2
MOE_HIDDEN = 512
MOE_ROWS = 256
ROUTER_LANES = 128

LANES = 128
VMEM_LIMIT = 52 * 1024 * 1024

BF16 = jnp.bfloat16
F32 = jnp.float32


def _cparams(n_axes):
    return pltpu.CompilerParams(dimension_semantics=("arbitrary",) * n_axes, vmem_limit_bytes=VMEM_LIMIT)


def _rms(x, g):
    return x * lax.rsqrt(jnp.mean(x * x, axis=-1, keepdims=True) + NORM_EPS) * g


def _dot(a, b):
    return jnp.dot(a, b, preferred_element_type=F32)


def _dot_nt(a, b):
    return lax.dot_general(a, b, (((1,), (1,)), ((), ())), preferred_element_type=F32)


def _rope(x, cos, sin):
    w = x.shape[1]
    lane = lax.broadcasted_iota(jnp.int32, x.shape, 1)
    first = (lane & (ROPE_DIM - 1)) < ROPE_DIM // 2
    partner = jnp.where(first, pltpu.roll(x, w - ROPE_DIM // 2, 1), pltpu.roll(x, ROPE_DIM // 2, 1))
    reps = w // LANES
    if reps > 1:
        cos = jnp.concatenate([cos] * reps, axis=1)
        sin = jnp.concatenate([sin] * reps, axis=1)
    return x * cos + partner * sin


def _rope_tables(seq_len):
    inv = ROPE_THETA ** (-jnp.arange(0, ROPE_DIM, 2, dtype=F32) / ROPE_DIM)
    ang = jnp.arange(seq_len, dtype=F32)[:, None] * inv[None, :]
    c, s = jnp.cos(ang), jnp.sin(ang)
    return jnp.concatenate([c, c, c, c], axis=1), jnp.concatenate([-s, s, -s, s], axis=1)


def _proj_res_kernel(x_ref, w_ref, h_ref, o_ref):
    o_ref[...] = h_ref[...] + _dot(x_ref[...], w_ref[...])


def _proj_res(x, w, h, tm=512):
    n, k = x.shape
    d = w.shape[1]
    tm = min(tm, n)
    return pl.pallas_call(
        _proj_res_kernel,
        grid=(n // tm,),
        in_specs=[pl.BlockSpec((tm, k), lambda i: (i, 0)), pl.BlockSpec((k, d), lambda i: (0, 0)),
                  pl.BlockSpec((tm, d), lambda i: (i, 0))],
        out_specs=pl.BlockSpec((tm, d), lambda i: (i, 0)),
        out_shape=jax.ShapeDtypeStruct((n, d), F32),
        compiler_params=_cparams(1),
        name="proj_res",
    )(x, w, h)


def _mla_proj_kernel(h_ref, g_ref, wa_ref, nq_ref, nkv_ref, wuq_ref, wukv_ref, cos_ref, sin_ref,
                     q_ref, k_ref, v_ref):
    scale = MLA_QK ** -0.5
    xn = _rms(h_ref[0], g_ref[...]).astype(BF16)
    a = _dot(xn, wa_ref[...])
    cq = _rms(a[:, :MLA_Q_LORA], nq_ref[...]).astype(BF16)
    ckv = _rms(a[:, MLA_Q_LORA:MLA_Q_LORA + MLA_KV_LORA], nkv_ref[...]).astype(BF16)
    cos, sin = cos_ref[...], sin_ref[...]
    kr = _rope(a[:, MLA_Q_LORA + MLA_KV_LORA:], cos, sin)[:, :ROPE_DIM].astype(BF16)
    qf = _dot(cq, wuq_ref[...]) * scale
    kvf = _dot(ckv, wukv_ref[...])
    qr = _rope(qf[:, MLA_HEADS * MLA_NOPE:], cos, sin)
    for hd in range(MLA_HEADS):
        q_ref[0, hd, :, :MLA_NOPE] = qf[:, hd * MLA_NOPE:(hd + 1) * MLA_NOPE].astype(BF16)
        q_ref[0, hd, :, MLA_NOPE:] = qr[:, hd * ROPE_DIM:(hd + 1) * ROPE_DIM].astype(BF16)
        base = hd * (MLA_NOPE + MLA_V)
        k_ref[0, hd, :, :MLA_NOPE] = kvf[:, base:base + MLA_NOPE].astype(BF16)
        k_ref[0, hd, :, MLA_NOPE:] = kr
        v_ref[0, hd] = kvf[:, base + MLA_NOPE:base + MLA_NOPE + MLA_V].astype(BF16)


def _mla_proj(h, g, w_a, norm_q, norm_kv, w_uq, w_ukv, cos, sin, tm=256):
    b, s, d = h.shape
    tm = min(tm, s)
    wa = jnp.pad(w_a, ((0, 0), (0, ROPE_DIM))).astype(BF16)
    wuq = w_uq.reshape(MLA_Q_LORA, MLA_HEADS, MLA_QK)
    wuq = jnp.concatenate([wuq[:, :, :MLA_NOPE].reshape(MLA_Q_LORA, -1), wuq[:, :, MLA_NOPE:].reshape(MLA_Q_LORA, -1)],
                          axis=1).astype(BF16)
    wukv = w_ukv.astype(BF16)
    const = lambda shape: pl.BlockSpec(shape, lambda bi, i: (0,) * len(shape))
    hspec = lambda width: pl.BlockSpec((1, MLA_HEADS, tm, width), lambda bi, i: (bi, 0, i, 0))
    return pl.pallas_call(
        _mla_proj_kernel,
        grid=(b, s // tm),
        in_specs=[pl.BlockSpec((1, tm, d), lambda bi, i: (bi, i, 0)), const((1, d)), const(wa.shape),
                  const((1, MLA_Q_LORA)), const((1, MLA_KV_LORA)), const(wuq.shape), const(wukv.shape),
                  pl.BlockSpec((tm, LANES), lambda bi, i: (i, 0)), pl.BlockSpec((tm, LANES), lambda bi, i: (i, 0))],
        out_specs=[hspec(MLA_QK), hspec(MLA_QK), hspec(MLA_V)],
        out_shape=[jax.ShapeDtypeStruct((b, MLA_HEADS, s, MLA_QK), BF16),
                   jax.ShapeDtypeStruct((b, MLA_HEADS, s, MLA_QK), BF16),
                   jax.ShapeDtypeStruct((b, MLA_HEADS, s, MLA_V), BF16)],
        compiler_params=_cparams(2),
        name="mla_proj",
    )(h, g.reshape(1, d), wa, norm_q.reshape(1, -1), norm_kv.reshape(1, -1), wuq, wukv, cos, sin)


def _mla_attn_kernel(q_ref, k_ref, v_ref, o_ref, m_sc, l_sc, acc_sc, *, tk):
    q = q_ref[0, 0]
    m_sc[...] = jnp.full(m_sc.shape, -jnp.inf, F32)
    l_sc[...] = jnp.zeros(l_sc.shape, F32)
    acc_sc[...] = jnp.zeros(acc_sc.shape, F32)

    def body(j, carry):
        start = pl.multiple_of(j * tk, tk)
        kc = k_ref[0, 0, pl.ds(start, tk), :]
        vc = v_ref[0, 0, pl.ds(start, tk), :]
        s = _dot_nt(q, kc)
        m_old = m_sc[...]
        m_new = jnp.maximum(m_old, jnp.max(s, axis=-1, keepdims=True))
        alpha = jnp.exp(m_old - m_new)
        p = jnp.exp(s - m_new[:, :1])
        l_sc[...] = alpha * l_sc[...] + jnp.sum(p, axis=-1, keepdims=True)
        acc_sc[...] = alpha * acc_sc[...] + _dot(p.astype(BF16), vc)
        m_sc[...] = m_new
        return carry

    lax.fori_loop(0, k_ref.shape[2] // tk, body, 0)
    o_ref[0] = (acc_sc[...] / l_sc[...]).astype(BF16)


def _mla_attn(q, k, v, tq=512, tk=1024):
    b, nh, s, _ = q.shape
    tq, tk = min(tq, s), min(tk, s)
    return pl.pallas_call(
        functools.partial(_mla_attn_kernel, tk=tk),
        grid=(b, nh, s // tq),
        in_specs=[pl.BlockSpec((1, 1, tq, MLA_QK), lambda bi, hi, i: (bi, hi, i, 0)),
                  pl.BlockSpec((1, 1, s, MLA_QK), lambda bi, hi, i: (bi, hi, 0, 0)),
                  pl.BlockSpec((1, 1, s, MLA_V), lambda bi, hi, i: (bi, hi, 0, 0))],
        out_specs=pl.BlockSpec((1, tq, MLA_V), lambda bi, hi, i: (bi, i, hi)),
        out_shape=jax.ShapeDtypeStruct((b, s, nh * MLA_V), BF16),
        scratch_shapes=[pltpu.VMEM((tq, MLA_V), F32), pltpu.VMEM((tq, MLA_V), F32), pltpu.VMEM((tq, MLA_V), F32)],
        compiler_params=_cparams(3),
        name="mla_attn",
    )(q, k, v)


def _mla_mixer(h, g, cos, sin, w_a, norm_q, norm_kv, w_uq, w_ukv, w_o):
    b, s, d = h.shape
    q, k, v = _mla_proj(h, g, w_a, norm_q, norm_kv, w_uq, w_ukv, cos, sin)
    o = _mla_attn(q, k, v)
    return _proj_res(o.reshape(b * s, -1), w_o.astype(BF16), h.reshape(b * s, d)).reshape(b, s, d)


def _fnet_chan_kernel(h_ref, g_ref, w_ref, ur_ref, ui_ref):
    xn = _rms(h_ref[...], g_ref[...]).astype(BF16)
    gw = D_MODEL // FNET_GROUPS
    for gi in range(FNET_GROUPS):
        u = _dot(xn[:, gi * gw:(gi + 1) * gw], w_ref[...])
        ur_ref[:, gi * gw:(gi + 1) * gw] = u[:, :gw].astype(BF16)
        ui_ref[:, gi * gw:(gi + 1) * gw] = u[:, gw:].astype(BF16)


def _fnet_stage1_kernel(m_ref, ur_ref, ui_ref, ar_ref, ai_ref):
    n1 = ur_ref.shape[1]
    x = jnp.concatenate([ur_ref[0], ui_ref[0]], axis=0)
    a = _dot(m_ref[...], x)
    ar_ref[0] = a[:n1].astype(BF16)
    ai_ref[0] = a[n1:].astype(BF16)


def _fnet_stage2_kernel(m_ref, ar_ref, ai_ref, h_ref, w_ref, b_ref, o_ref, y_sc, *, kg):
    n2 = ar_ref.shape[2]
    d = w_ref.shape[0]
    for gi in range(kg):
        x = jnp.concatenate([ar_ref[0, gi], ai_ref[0, gi]], axis=0)
        y_sc[gi * n2:(gi + 1) * n2, :] = _dot(m_ref[gi], x).astype(BF16)
    o = _dot(y_sc[...], w_ref[...])
    for gi in range(kg):
        o_ref[0, :, gi * d:(gi + 1) * d] = h_ref[0, :, gi * d:(gi + 1) * d] + o[gi * n2:(gi + 1) * n2] + b_ref[...]


def _fnet_tables(seq_len):
    gw = D_MODEL // FNET_GROUPS
    n1, n2 = FNET_N1, seq_len // FNET_N1
    c = np.arange(gw)
    ang = 2 * np.pi * np.outer(c, c) / gw
    wc = np.concatenate([np.cos(ang), -np.sin(ang)], axis=1) / math.sqrt(gw)
    k1 = np.arange(n1)
    a1 = 2 * np.pi * np.outer(k1, k1) / n1
    c1, s1 = np.cos(a1), np.sin(a1)
    m1 = np.block([[c1, s1], [-s1, c1]]) / math.sqrt(n1)
    s2 = np.arange(n2)
    th = 2 * np.pi * (np.outer(s2, s2)[None] / n2 + k1[:, None, None] * s2[None, None, :] / seq_len)
    m2 = np.concatenate([np.cos(th), np.sin(th)], axis=2) / math.sqrt(n2)
    return (jnp.asarray(wc, dtype=BF16), jnp.asarray(m1, dtype=BF16), jnp.asarray(m2, dtype=BF16))


def _fnet_mixer(h, g, w_o, b_o, tm=512, cols=4, kg=4):
    b, s, d = h.shape
    n = b * s
    n1, n2 = FNET_N1, s // FNET_N1
    tm, cols, kg = min(tm, n), min(cols, n2), min(kg, n1)
    wc, m1, m2 = _fnet_tables(s)
    gw = d // FNET_GROUPS
    ur, ui = pl.pallas_call(
        _fnet_chan_kernel,
        grid=(n // tm,),
        in_specs=[pl.BlockSpec((tm, d), lambda i: (i, 0)), pl.BlockSpec((1, d), lambda i: (0, 0)),
                  pl.BlockSpec((gw, 2 * gw), lambda i: (0, 0))],
        out_specs=[pl.BlockSpec((tm, d), lambda i: (i, 0))] * 2,
        out_shape=[jax.ShapeDtypeStruct((n, d), BF16)] * 2,
        compiler_params=_cparams(1),
        name="fnet_chan",
    )(h.reshape(n, d), g.reshape(1, d), wc)
    blk = pl.BlockSpec((1, n1, cols * d), lambda bi, j: (bi, 0, j))
    ar, ai = pl.pallas_call(
        _fnet_stage1_kernel,
        grid=(b, n2 // cols),
        in_specs=[pl.BlockSpec((2 * n1, 2 * n1), lambda bi, j: (0, 0)), blk, blk],
        out_specs=[blk, blk],
        out_shape=[jax.ShapeDtypeStruct((b, n1, n2 * d), BF16)] * 2,
        compiler_params=_cparams(2),
        name="fnet_stage1",
    )(m1, ur.reshape(b, n1, n2 * d), ui.reshape(b, n1, n2 * d))
    ablk = pl.BlockSpec((1, kg, n2, d), lambda bi, j: (bi, j, 0, 0))
    hblk = pl.BlockSpec((1, n2, kg * d), lambda bi, j: (bi, 0, j))
    out = pl.pallas_call(
        functools.partial(_fnet_stage2_kernel, kg=kg),
        grid=(b, n1 // kg),
        in_specs=[pl.BlockSpec((kg, n2, 2 * n2), lambda bi, j: (j, 0, 0)), ablk, ablk, hblk,
                  pl.BlockSpec((d, d), lambda bi, j: (0, 0)), pl.BlockSpec((1, d), lambda bi, j: (0, 0))],
        out_specs=hblk,
        out_shape=jax.ShapeDtypeStruct((b, n2, n1 * d), F32),
        scratch_shapes=[pltpu.VMEM((kg * n2, d), BF16)],
        compiler_params=_cparams(2),
        name="fnet_stage2",
    )(m2, ar.reshape(b, n1, n2, d), ai.reshape(b, n1, n2, d), h.reshape(b, n2, n1 * d), w_o.astype(BF16),
      b_o.reshape(1, d))
    return out.reshape(b, s, d)


def _gelu(x):
    return 0.5 * x * (1.0 + jnp.tanh(math.sqrt(2.0 / math.pi) * (x + 0.044715 * (x * x * x))))


def _gmlp_in_kernel(h_ref, g_ref, w_ref, nv_ref, u_ref, v_ref):
    xn = _rms(h_ref[...], g_ref[...]).astype(BF16)
    u_ref[...] = _gelu(_dot(xn, w_ref[:, :GMLP_HALF])).astype(BF16)
    v = _gelu(_dot(xn, w_ref[:, GMLP_HALF:]))
    v_ref[...] = _rms(v, nv_ref[...]).astype(BF16)


def _gmlp_gate_kernel(u_ref, v_ref, ws_ref, bs_ref, h_ref, wo_ref, o_ref, gate_sc):
    for ci in range(u_ref.shape[0] // GMLP_CHUNK):
        rows = slice(ci * GMLP_CHUNK, (ci + 1) * GMLP_CHUNK)
        for gi in range(GMLP_GROUPS):
            cols = slice(gi * GMLP_GROUP_DIM, (gi + 1) * GMLP_GROUP_DIM)
            sv = _dot(ws_ref[gi], v_ref[rows, cols]) + bs_ref[gi]
            gate_sc[rows, cols] = (u_ref[rows, cols].astype(F32) * sv).astype(BF16)
    o_ref[...] = h_ref[...] + _dot(gate_sc[...], wo_ref[...])


def _gmlp_mixer(h, g, w_in, norm_v, w_s, b_s, w_o, tm_in=256, tm=512):
    b, s, d = h.shape
    n = b * s
    tm_in, tm = min(tm_in, n), min(tm, n)
    h2 = h.reshape(n, d)
    u, v = pl.pallas_call(
        _gmlp_in_kernel,
        grid=(n // tm_in,),
        in_specs=[pl.BlockSpec((tm_in, d), lambda i: (i, 0)), pl.BlockSpec((1, d), lambda i: (0, 0)),
                  pl.BlockSpec((d, 2 * GMLP_HALF), lambda i: (0, 0)), pl.BlockSpec((1, GMLP_HALF), lambda i: (0, 0))],
        out_specs=[pl.BlockSpec((tm_in, GMLP_HALF), lambda i: (i, 0))] * 2,
        out_shape=[jax.ShapeDtypeStruct((n, GMLP_HALF), BF16)] * 2,
        compiler_params=_cparams(1),
        name="gmlp_in",
    )(h2, g.reshape(1, d), w_in.astype(BF16), norm_v.reshape(1, -1))
    bsb = jnp.broadcast_to(b_s[:, :, None], (GMLP_GROUPS, GMLP_CHUNK, GMLP_GROUP_DIM)).astype(F32)
    out = pl.pallas_call(
        _gmlp_gate_kernel,
        grid=(n // tm,),
        in_specs=[pl.BlockSpec((tm, GMLP_HALF), lambda i: (i, 0)), pl.BlockSpec((tm, GMLP_HALF), lambda i: (i, 0)),
                  pl.BlockSpec((GMLP_GROUPS, GMLP_CHUNK, GMLP_CHUNK), lambda i: (0, 0, 0)),
                  pl.BlockSpec((GMLP_GROUPS, GMLP_CHUNK, GMLP_GROUP_DIM), lambda i: (0, 0, 0)),
                  pl.BlockSpec((tm, d), lambda i: (i, 0)), pl.BlockSpec((GMLP_HALF, d), lambda i: (0, 0))],
        out_specs=pl.BlockSpec((tm, d), lambda i: (i, 0)),
        out_shape=jax.ShapeDtypeStruct((n, d), F32),
        scratch_shapes=[pltpu.VMEM((tm, GMLP_HALF), BF16)],
        compiler_params=_cparams(1),
        name="gmlp_gate",
    )(u, v, w_s.astype(BF16), bsb, h2, w_o.astype(BF16))
    return out.reshape(b, s, d)


def _dil_qkv_kernel(h_ref, g_ref, w_ref, cos_ref, sin_ref, o_ref):
    xn = _rms(h_ref[0], g_ref[...]).astype(BF16)
    cos, sin = cos_ref[...], sin_ref[...]
    for j in range(DIL_GROUPS * 3):
        cols = slice(j * DIL_WIDTH, (j + 1) * DIL_WIDTH)
        y = _dot(xn, w_ref[:, cols])
        if j % 3 == 0:
            y = _rope(y, cos, sin) * DIL_HEAD_DIM ** -0.5
        elif j % 3 == 1:
            y = _rope(y, cos, sin)
        o_ref[0, :, cols] = y.astype(BF16)


def _dil_attn_kernel(q_ref, kp_ref, kc_ref, kn_ref, vp_ref, vc_ref, vn_ref, o_ref, lse_ref, *, seq_len):
    tq = q_ref.shape[1]
    l0 = pl.program_id(2) * tq
    k = jnp.concatenate([kp_ref[0], kc_ref[0], kn_ref[0]], axis=0)
    v = jnp.concatenate([vp_ref[0], vc_ref[0], vn_ref[0]], axis=0)
    nk = k.shape[0]
    pad = kp_ref.shape[1]
    row = lax.broadcasted_iota(jnp.int32, (tq, nk), 0)
    col = lax.broadcasted_iota(jnp.int32, (tq, nk), 1)
    kpos = l0 - pad + col
    rel = col - pad - row
    valid = (rel <= DIL_BAND) & (rel >= -DIL_BAND) & (kpos >= 0) & (kpos < seq_len)
    outs, lses = [], []
    for hd in range(DIL_HEADS):
        cols = slice(hd * DIL_HEAD_DIM, (hd + 1) * DIL_HEAD_DIM)
        s = jnp.where(valid, _dot_nt(q_ref[0, :, cols], k[:, cols]), -1e30)
        m = jnp.max(s, axis=-1, keepdims=True)
        p = jnp.exp(s - m)
        l = jnp.sum(p, axis=-1, keepdims=True)
        outs.append(_dot(p.astype(BF16), v[:, cols]) / l)
        lses.append(jnp.broadcast_to(m + jnp.log(l), (tq, DIL_HEAD_DIM)))
    o_ref[0] = jnp.concatenate(outs, axis=1).astype(BF16)
    lse_ref[0] = jnp.concatenate(lses, axis=1)


def _dil_mix_kernel(o0_ref, o1_ref, o2_ref, l0_ref, l1_ref, l2_ref, h_ref, w_ref, out_ref):
    l0, l1, l2 = l0_ref[...], l1_ref[...], l2_ref[...]
    mx = jnp.maximum(jnp.maximum(l0, l1), l2)
    e0, e1, e2 = jnp.exp(l0 - mx), jnp.exp(l1 - mx), jnp.exp(l2 - mx)
    o = (e0 * o0_ref[...].astype(F32) + e1 * o1_ref[...].astype(F32) + e2 * o2_ref[...].astype(F32)) / (e0 + e1 + e2)
    out_ref[...] = h_ref[...] + _dot(o.astype(BF16), w_ref[...])


def _dil_mixer(h, g, cos, sin, w_qkv, w_o, tm=256, tq=128):
    b, s, d = h.shape
    n = b * s
    tm = min(tm, s)
    nqkv = DIL_GROUPS * 3
    qkv = pl.pallas_call(
        _dil_qkv_kernel,
        grid=(b, s // tm),
        in_specs=[pl.BlockSpec((1, tm, d), lambda bi, i: (bi, i, 0)), pl.BlockSpec((1, d), lambda bi, i: (0, 0)),
                  pl.BlockSpec((d, nqkv * DIL_WIDTH), lambda bi, i: (0, 0)),
                  pl.BlockSpec((tm, LANES), lambda bi, i: (i, 0)), pl.BlockSpec((tm, LANES), lambda bi, i: (i, 0))],
        out_specs=pl.BlockSpec((1, tm, nqkv * DIL_WIDTH), lambda bi, i: (bi, i, 0)),
        out_shape=jax.ShapeDtypeStruct((b, s, nqkv * DIL_WIDTH), BF16),
        compiler_params=_cparams(2),
        name="dil_qkv",
    )(h, g.reshape(1, d), w_qkv.astype(BF16), cos, sin)
    outs, lses = [], []
    for gi, (window, dil) in enumerate(DIL_PATTERNS):
        assert window // (2 * dil) == DIL_BAND
        ln = s // dil
        tql = min(tq, ln)
        pad = min(LANES, ln)
        assert pad >= DIL_BAND and tql % pad == 0 and ln % tql == 0
        ratio = tql // pad
        nblk = ln // pad
        view = qkv.reshape(b, ln, dil * nqkv * DIL_WIDTH)

        def cur(j, gi=gi):
            return pl.BlockSpec((1, tql, DIL_WIDTH), lambda bi, r, i: (bi, i, r * nqkv + gi * 3 + j))

        def prev(j, gi=gi, ratio=ratio):
            return pl.BlockSpec((1, pad, DIL_WIDTH),
                                lambda bi, r, i: (bi, jnp.maximum(i * ratio - 1, 0), r * nqkv + gi * 3 + j))

        def nxt(j, gi=gi, ratio=ratio, nblk=nblk):
            return pl.BlockSpec((1, pad, DIL_WIDTH),
                                lambda bi, r, i: (bi, jnp.minimum((i + 1) * ratio, nblk - 1), r * nqkv + gi * 3 + j))

        oblk = pl.BlockSpec((1, tql, DIL_WIDTH), lambda bi, r, i: (bi, i, r))
        o, lse = pl.pallas_call(
            functools.partial(_dil_attn_kernel, seq_len=ln),
            grid=(b, dil, ln // tql),
            in_specs=[cur(0), prev(1), cur(1), nxt(1), prev(2), cur(2), nxt(2)],
            out_specs=[oblk, oblk],
            out_shape=[jax.ShapeDtypeStruct((b, ln, dil * DIL_WIDTH), BF16),
                       jax.ShapeDtypeStruct((b, ln, dil * DIL_WIDTH), F32)],
            compiler_params=_cparams(3),
            name="dil_attn_%d" % gi,
        )(view, view, view, view, view, view, view)
        outs.append(o.reshape(n, DIL_WIDTH))
        lses.append(lse.reshape(n, DIL_WIDTH))
    tmx = min(512, n)
    row = lambda width: pl.BlockSpec((tmx, width), lambda i: (i, 0))
    out = pl.pallas_call(
        _dil_mix_kernel,
        grid=(n // tmx,),
        in_specs=[row(DIL_WIDTH)] * 6 + [row(d), pl.BlockSpec((DIL_WIDTH, d), lambda i: (0, 0))],
        out_specs=row(d),
        out_shape=jax.ShapeDtypeStruct((n, d), F32),
        compiler_params=_cparams(1),
        name="dil_mix",
    )(*outs, *lses, h.reshape(n, d), w_o.astype(BF16))
    return out.reshape(b, s, d)


def _split_bf16(x):
    hi = x.astype(BF16)
    return hi, (x - hi.astype(F32)).astype(BF16)


def _router_kernel(h_ref, g_ref, whi_ref, wlo_ref, b_ref, xn_ref, lg_ref):
    xn = _rms(h_ref[...], g_ref[...])
    xhi, xlo = _split_bf16(xn)
    xn_ref[...] = xhi
    lg_ref[...] = _dot(xhi, whi_ref[...]) + _dot(xlo, whi_ref[...]) + _dot(xhi, wlo_ref[...]) + b_ref[...]


def _moe_router(h2, g, w_grp, b_grp, w_exp, b_exp, tm=512):
    n, d = h2.shape
    tm = min(tm, n)
    nlog = MOE_GROUPS + MOE_EXPERTS
    w = jnp.pad(jnp.concatenate([w_grp, w_exp], axis=1), ((0, 0), (0, ROUTER_LANES - nlog)))
    bias = jnp.pad(jnp.concatenate([b_grp, b_exp]), (0, ROUTER_LANES - nlog)).reshape(1, ROUTER_LANES)
    whi, wlo = _split_bf16(w)
    return pl.pallas_call(
        _router_kernel,
        grid=(n // tm,),
        in_specs=[pl.BlockSpec((tm, d), lambda i: (i, 0)), pl.BlockSpec((1, d), lambda i: (0, 0)),
                  pl.BlockSpec((d, ROUTER_LANES), lambda i: (0, 0)), pl.BlockSpec((d, ROUTER_LANES), lambda i: (0, 0)),
                  pl.BlockSpec((1, ROUTER_LANES), lambda i: (0, 0))],
        out_specs=[pl.BlockSpec((tm, d), lambda i: (i, 0)), pl.BlockSpec((tm, ROUTER_LANES), lambda i: (i, 0))],
        out_shape=[jax.ShapeDtypeStruct((n, d), BF16), jax.ShapeDtypeStruct((n, ROUTER_LANES), F32)],
        compiler_params=_cparams(1),
        name="moe_router",
    )(h2, g.reshape(1, d), whi, wlo, bias)


def _moe_expert_kernel(be_ref, nu_ref, x_ref, rw_ref, wg_ref, wu_ref, wd_ref, y_ref):
    live = pl.program_id(0) < nu_ref[0]

    @pl.when(live)
    def _():
        x = x_ref[...]
        gt = _dot(x, wg_ref[0])
        up = _dot(x, wu_ref[0])
        hb = (gt * (1.0 / (1.0 + jnp.exp(-gt))) * up).astype(BF16)
        y_ref[...] = (_dot(hb, wd_ref[0]) * rw_ref[...]).astype(BF16)

    @pl.when(jnp.logical_not(live))
    def _():
        y_ref[...] = jnp.zeros(y_ref.shape, BF16)


def _moe_experts(x_rows, row_w, blk_e, n_used, w_gate, w_up, w_down):
    r, d = x_rows.shape
    nb = r // MOE_ROWS
    last = lambda i, nu: jnp.minimum(i, nu[0] - 1)
    grid_spec = pltpu.PrefetchScalarGridSpec(
        num_scalar_prefetch=2,
        grid=(nb,),
        in_specs=[pl.BlockSpec((MOE_ROWS, d), lambda i, be, nu: (last(i, nu), 0)),
                  pl.BlockSpec((MOE_ROWS, 1), lambda i, be, nu: (last(i, nu), 0)),
                  pl.BlockSpec((1, d, MOE_HIDDEN), lambda i, be, nu: (be[last(i, nu)], 0, 0)),
                  pl.BlockSpec((1, d, MOE_HIDDEN), lambda i, be, nu: (be[last(i, nu)], 0, 0)),
                  pl.BlockSpec((1, MOE_HIDDEN, d), lambda i, be, nu: (be[last(i, nu)], 0, 0))],
        out_specs=pl.BlockSpec((MOE_ROWS, d), lambda i, be, nu: (i, 0)),
    )
    return pl.pallas_call(
        _moe_expert_kernel,
        grid_spec=grid_spec,
        out_shape=jax.ShapeDtypeStruct((r, d), BF16),
        compiler_params=_cparams(1),
        name="moe_experts",
    )(blk_e, n_used, x_rows, row_w.reshape(r, 1), w_gate, w_up, w_down)


def _moe_plan(logits):
    n = logits.shape[0]
    nk = n * MOE_TOP_K
    pg = jax.nn.softmax(logits[:, :MOE_GROUPS], axis=-1)
    grp = jnp.argmax(pg, axis=-1)
    g_gate = jnp.take_along_axis(pg, grp[:, None], axis=1)[:, 0]
    le = logits[:, MOE_GROUPS:MOE_GROUPS + MOE_EXPERTS].reshape(n, MOE_GROUPS, MOE_PER_GROUP)
    le_g = jnp.take_along_axis(le, grp[:, None, None], axis=1)[:, 0]
    top_v, top_i = lax.top_k(le_g, MOE_TOP_K)
    wts = (jax.nn.softmax(top_v, axis=-1) * g_gate[:, None]).reshape(-1)
    eid = (grp[:, None] * MOE_PER_GROUP + top_i).reshape(-1).astype(jnp.int32)
    order = jnp.argsort(eid).astype(jnp.int32)
    inv_order = jnp.argsort(order).astype(jnp.int32)
    counts = jnp.sum((eid[:, None] == jnp.arange(MOE_EXPERTS, dtype=jnp.int32)[None, :]).astype(jnp.int32), axis=0)
    padded = ((counts + MOE_ROWS - 1) // MOE_ROWS) * MOE_ROWS
    pad_end = jnp.cumsum(padded)
    pad_start = pad_end - padded
    start = jnp.cumsum(counts) - counts
    nrows = ((nk + MOE_EXPERTS * (MOE_ROWS - 1) + MOE_ROWS - 1) // MOE_ROWS) * MOE_ROWS
    nb = nrows // MOE_ROWS
    blk_e = jnp.minimum(jnp.searchsorted(pad_end, jnp.arange(nb, dtype=jnp.int32) * MOE_ROWS, side='right'),
                        MOE_EXPERTS - 1).astype(jnp.int32)
    n_used = (pad_end[-1] // MOE_ROWS).astype(jnp.int32).reshape(1)
    row_e = jnp.repeat(blk_e, MOE_ROWS)
    off = jnp.arange(nrows, dtype=jnp.int32) - pad_start[row_e]
    live = off < counts[row_e]
    src = order[jnp.clip(start[row_e] + off, 0, nk - 1)]
    row_tok = jnp.where(live, src // MOE_TOP_K, 0)
    row_w = jnp.where(live, wts[src], 0.0)
    se = eid
    dest = pad_start[se] + (inv_order - start[se])
    return row_tok, row_w, blk_e, n_used, dest.reshape(n, MOE_TOP_K)


def _ple_kernel(h_ref, y1_ref, y2_ref, p_ref, g_ref, wg_ref, wp_ref, gf_ref, o_ref, *, final):
    h = h_ref[...] + y1_ref[...].astype(F32) + y2_ref[...].astype(F32)
    gate = _dot(_rms(h, g_ref[...]).astype(BF16), wg_ref[...])
    gate = 1.0 / (1.0 + jnp.exp(-gate))
    out = h + gate * _dot(p_ref[...].astype(BF16), wp_ref[...])
    if final:
        out = _rms(out, gf_ref[...])
    o_ref[...] = out


def _moe_ple(h2, y1, y2, p2, g, w_gate, w_proj, g_final, final, tm=512):
    n, d = h2.shape
    tm = min(tm, n)
    row = lambda width: pl.BlockSpec((tm, width), lambda i: (i, 0))
    const = lambda a, c: pl.BlockSpec((a, c), lambda i: (0, 0))
    return pl.pallas_call(
        functools.partial(_ple_kernel, final=final),
        grid=(n // tm,),
        in_specs=[row(d), row(d), row(d), row(PLE_DIM), const(1, d), const(d, d), const(PLE_DIM, d), const(1, d)],
        out_specs=row(d),
        out_shape=jax.ShapeDtypeStruct((n, d), F32),
        compiler_params=_cparams(1),
        name="moe_ple",
    )(h2, y1, y2, p2, g.reshape(1, d), w_gate.astype(BF16), w_proj.astype(BF16), g_final.reshape(1, d))


def _moe_ple_layer(h, p, g_moe, w_grp, b_grp, w_exp, b_exp, w_gate, w_up, w_down, g_ple, ple_w_gate, ple_w_proj,
                   g_final, final):
    b, s, d = h.shape
    n = b * s
    h2 = h.reshape(n, d)
    xn, logits = _moe_router(h2, g_moe, w_grp, b_grp, w_exp, b_exp)
    row_tok, row_w, blk_e, n_used, dest = _moe_plan(logits)
    x_rows = jnp.take(xn, row_tok, axis=0)
    y = _moe_experts(x_rows, row_w, blk_e, n_used, w_gate.astype(BF16), w_up.astype(BF16), w_down.astype(BF16))
    y1 = jnp.take(y, dest[:, 0], axis=0)
    y2 = jnp.take(y, dest[:, 1], axis=0)
    out = _moe_ple(h2, y1, y2, p.reshape(n, -1), g_ple, ple_w_gate, ple_w_proj, g_final, final)
    return out.reshape(b, s, d)


def kernel(x, p, norm_mix, norm_moe, mla_w_a, mla_norm_q, mla_norm_kv, mla_w_uq, mla_w_ukv, mla_w_o, fnet_w_o, fnet_b_o, gmlp_w_in, gmlp_norm_v, gmlp_w_s, gmlp_b_s, gmlp_w_o, dil_w_qkv, dil_w_o, moe_w_grp, moe_b_grp, moe_w_exp, moe_b_exp, moe_w_gate, moe_w_up, moe_w_down, ple_norm, ple_w_gate, ple_w_proj, norm_final):
    depth = p.shape[0]
    cos, sin = _rope_tables(x.shape[1])
    h = x
    for i in range(depth):
        m, j = i % 4, i // 4
        if m == 0:
            h = _mla_mixer(h, norm_mix[i], cos, sin, mla_w_a[j], mla_norm_q[j], mla_norm_kv[j], mla_w_uq[j],
                           mla_w_ukv[j], mla_w_o[j])
        elif m == 1:
            h = _fnet_mixer(h, norm_mix[i], fnet_w_o[j], fnet_b_o[j])
        elif m == 2:
            h = _gmlp_mixer(h, norm_mix[i], gmlp_w_in[j], gmlp_norm_v[j], gmlp_w_s[j], gmlp_b_s[j], gmlp_w_o[j])
        else:
            h = _dil_mixer(h, norm_mix[i], cos, sin, dil_w_qkv[j], dil_w_o[j])
        h = _moe_ple_layer(h, p[i], norm_moe[i], moe_w_grp[i], moe_b_grp[i], moe_w_exp[i], moe_b_exp[i],
                           moe_w_gate[i], moe_w_up[i], moe_w_down[i], ple_norm[i], ple_w_gate[i], ple_w_proj[i],
                           norm_final, i == depth - 1)
    return h
```
